```python
import jax, jax.numpy as jnp
from jax import lax
import numpy as np

D_MODEL = 1024
BATCH = 32
SEQ = 2048
DEPTH = 4

CHUNK = 64
N_META = 16
POOL_WINDOWS = (2, 4, 8, 16)
N_POOL_GROUPS = len(POOL_WINDOWS)
POOL_GROUP = D_MODEL // N_POOL_GROUPS
CONV_WIDTH = 3
D_FF = 7 * D_MODEL // 2
N_EXPERTS = 8
TOP_K = 2
EPS = 1e-6
N_POOL_LAYERS = (DEPTH + 1) // 2
N_CONV_LAYERS = DEPTH // 2

kernel_name = "hybrid_pool_shortconv_moe_encoder"


def rms_norm(x, g):
    xf = x.astype(jnp.float32)
    y = xf * lax.rsqrt(jnp.mean(xf * xf, axis=-1, keepdims=True) + EPS)
    return (y * g.astype(jnp.float32)).astype(x.dtype)


def pool_mixer(h, w_grp, scale):
    T = h.shape[1]
    hf = h.astype(jnp.float32)
    cs = jnp.cumsum(hf, axis=1)
    pos = jnp.arange(1, T + 1, dtype=jnp.float32)[:, None]
    outs = []
    for g, w in enumerate(POOL_WINDOWS):
        sl = slice(g * POOL_GROUP, (g + 1) * POOL_GROUP)
        c = cs[..., sl]
        lower = jnp.pad(c, ((0, 0), (w, 0), (0, 0)))[:, :T]
        mean = (c - lower) / jnp.minimum(pos, float(w))
        d = (mean - hf[..., sl]).astype(h.dtype)
        outs.append(jnp.einsum('btc,cd->btd', d, w_grp[g]))
    return jnp.concatenate(outs, axis=-1) * scale


def conv_mixer(h, w_in, conv_k, w_out):
    bcv = jnp.einsum('btd,de->bte', h, w_in)
    b, c, v = jnp.split(bcv, 3, axis=-1)
    u = c * v
    y = lax.conv_general_dilated(
        u, conv_k[:, None, :], window_strides=(1,), padding=[(CONV_WIDTH - 1, 0)],
        dimension_numbers=('NWC', 'WIO', 'NWC'), feature_group_count=u.shape[-1])
    return jnp.einsum('btd,de->bte', b * y, w_out)


def swiglu(h, w_gate, w_up, w_down):
    a = jnp.einsum('btd,df->btf', h, w_gate)
    u = jnp.einsum('btd,df->btf', h, w_up)
    return jnp.einsum('btf,fd->btd', jax.nn.silu(a) * u, w_down)


def moe_swiglu(h, router, w_gate, w_up, w_down):
    logits = jnp.einsum('btd,de->bte', h, router).astype(jnp.float32)
    top_v, top_i = lax.top_k(logits, TOP_K)
    top_w = jax.nn.softmax(top_v, axis=-1)
    gates = jnp.sum(jax.nn.one_hot(top_i, N_EXPERTS, dtype=jnp.float32) * top_w[..., None], axis=-2)
    gates = gates.astype(h.dtype)
    out = jnp.zeros_like(h)
    for e in range(N_EXPERTS):
        out = out + gates[..., e:e + 1] * swiglu(h, w_gate[e], w_up[e], w_down[e])
    return out


def setup_inputs(seed: int = 0) -> dict:
    key = jax.random.key(seed)
    ks = jax.random.split(key, 20)
    D, F, E, G, Dg = D_MODEL, D_FF, N_EXPERTS, N_POOL_GROUPS, POOL_GROUP
    Lp, Lc = N_POOL_LAYERS, N_CONV_LAYERS
    nrm = lambda k, s, sc: jax.random.normal(k, s, jnp.float32) * sc
    gain = lambda k, s: 1.0 + 0.05 * jax.random.normal(k, s, jnp.float32)
    return {
        "x": nrm(ks[0], (BATCH, SEQ, D), 1.0),
        "meta_tokens": nrm(ks[1], (N_META, D), 1.0),
        "pool_norm": gain(ks[2], (Lp, D)),
        "pool_w": nrm(ks[3], (Lp, G, Dg, Dg), Dg ** -0.5),
        "pool_scale": gain(ks[4], (Lp, D)),
        "dense_norm": gain(ks[5], (Lp, D)),
        "dense_w_gate": nrm(ks[6], (Lp, D, F), D ** -0.5),
        "dense_w_up": nrm(ks[7], (Lp, D, F), D ** -0.5),
        "dense_w_down": nrm(ks[8], (Lp, F, D), F ** -0.5),
        "conv_norm": gain(ks[9], (Lc, D)),
        "conv_w_in": nrm(ks[10], (Lc, D, 3 * D), D ** -0.5),
        "conv_kernel": nrm(ks[11], (Lc, CONV_WIDTH, D), CONV_WIDTH ** -0.5),
        "conv_w_out": nrm(ks[12], (Lc, D, D), D ** -0.5),
        "moe_norm": gain(ks[13], (Lc, D)),
        "moe_router": nrm(ks[14], (Lc, D, E), D ** -0.5),
        "moe_w_gate": nrm(ks[15], (Lc, E, D, F), D ** -0.5),
        "moe_w_up": nrm(ks[16], (Lc, E, D, F), D ** -0.5),
        "moe_w_down": nrm(ks[17], (Lc, E, F, D), F ** -0.5),
        "final_norm": gain(ks[18], (D,)),
    }


def reference(x, meta_tokens, pool_norm, pool_w, pool_scale, dense_norm, dense_w_gate, dense_w_up,
              dense_w_down, conv_norm, conv_w_in, conv_kernel, conv_w_out, moe_norm, moe_router,
              moe_w_gate, moe_w_up, moe_w_down, final_norm):
    B = x.shape[0]
    meta = jnp.broadcast_to(meta_tokens[None].astype(x.dtype), (B, N_META, x.shape[-1]))
    h = jnp.concatenate([meta, x], axis=1)
    for i in range(DEPTH):
        j = i // 2
        if i % 2 == 0:
            h = h + pool_mixer(rms_norm(h, pool_norm[j]), pool_w[j], pool_scale[j])
            h = h + swiglu(rms_norm(h, dense_norm[j]), dense_w_gate[j], dense_w_up[j], dense_w_down[j])
        else:
            h = h + conv_mixer(rms_norm(h, conv_norm[j]), conv_w_in[j], conv_kernel[j], conv_w_out[j])
            h = h + moe_swiglu(rms_norm(h, moe_norm[j]), moe_router[j], moe_w_gate[j], moe_w_up[j],
                               moe_w_down[j])
    out = rms_norm(h, final_norm)
    return out[:, N_META:]
```

```python
import functools

import jax
import jax.numpy as jnp
from jax import lax
from jax.experimental import pallas as pl
from jax.experimental.pallas import tpu as pltpu

F32 = jnp.float32
BF16 = jnp.bfloat16
I32 = jnp.int32

D_MODEL = 1024
D_FF = 3584
N_EXPERTS = 8
N_META = 16
POOL_WINDOWS = (2, 4, 8, 16)
POOL_GROUP = D_MODEL // len(POOL_WINDOWS)
MAX_WINDOW = max(POOL_WINDOWS)
CONV_WIDTH = 3
CONV_HALO = 8
EPS = 1e-6
LANES = 128

SEQ_TILE = 688
MOE_TILE = 512
FF_CHUNK = 1792
FF_SUB = 896
VMEM_LIMIT = 56 * 1024 * 1024


def _rms(x, g):
    return x * lax.rsqrt(jnp.mean(x * x, axis=-1, keepdims=True) + EPS) * g


def _cparams(n_axes):
    return pltpu.CompilerParams(dimension_semantics=("arbitrary",) * n_axes,
                                vmem_limit_bytes=VMEM_LIMIT)


def _gather_rows(tile, pos_hbm, y_hbm, pos_smem, ybuf, sem):
    pltpu.sync_copy(pos_hbm.at[tile], pos_smem)

    def row_copy(t, k, p):
        return pltpu.make_async_copy(y_hbm.at[pl.ds(p, 1)], ybuf.at[k, pl.ds(t, 1)], sem)

    def issue(t, c):
        for k in range(2):
            row_copy(t, k, pos_smem[k * SEQ_TILE + t]).start()
        return c

    lax.fori_loop(0, SEQ_TILE, issue, 0)

    def drain(t, c):
        for k in range(2):
            row_copy(t, k, 0).wait()
        return c

    lax.fori_loop(0, SEQ_TILE, drain, 0)


def _combined(h, gates, ybuf):
    return h + gates[:, 4:5] * ybuf[0] + gates[:, 5:6] * ybuf[1]


def _pool_body(h, g_ref, w_ref, sc_ref, o_ref, ext_ref):
    s = pl.program_id(1)

    @pl.when(s == 0)
    def _():
        ext_ref[0:MAX_WINDOW, :] = jnp.zeros((MAX_WINDOW, D_MODEL), F32)

    xn = _rms(h, g_ref[...])
    ext_ref[MAX_WINDOW:, :] = xn
    seen = (s * SEQ_TILE + 1 + lax.broadcasted_iota(I32, (SEQ_TILE, 1), 0)).astype(F32)
    outs = []
    for g, w in enumerate(POOL_WINDOWS):
        lo, hi = g * POOL_GROUP, (g + 1) * POOL_GROUP
        acc = ext_ref[:, lo:hi]
        k = 1
        while k < w:
            acc = acc + pltpu.roll(acc, k, 0)
            k *= 2
        win = acc[MAX_WINDOW:, :]
        mean = win * (1.0 / jnp.minimum(seen, float(w)))
        d = (mean - xn[:, lo:hi]).astype(BF16)
        outs.append(jnp.dot(d, w_ref[g], preferred_element_type=F32))
    mix = jnp.concatenate(outs, axis=-1) * sc_ref[...]
    o_ref[0] = h + mix
    ext_ref[0:MAX_WINDOW, :] = xn[SEQ_TILE - MAX_WINDOW:, :]


def _pool_kernel(h_ref, g_ref, w_ref, sc_ref, o_ref, ext_ref):
    _pool_body(h_ref[0], g_ref, w_ref, sc_ref, o_ref, ext_ref)


def _pool_combine_kernel(h_ref, gates_ref, pos_hbm, y_hbm, g_ref, w_ref, sc_ref, o_ref,
                         ext_ref, pos_smem, ybuf, sem):
    tile = pl.program_id(0) * pl.num_programs(1) + pl.program_id(1)
    _gather_rows(tile, pos_hbm, y_hbm, pos_smem, ybuf, sem)
    h = _combined(h_ref[0], gates_ref[...], ybuf)
    _pool_body(h, g_ref, w_ref, sc_ref, o_ref, ext_ref)


def _pool_layer(h, norm_g, w_bf, scale, moe=None):
    B, T, D = h.shape
    ns = T // SEQ_TILE
    h_spec = pl.BlockSpec((1, SEQ_TILE, D), lambda b, s: (b, s, 0))
    vec_spec = pl.BlockSpec((1, D), lambda b, s: (0, 0))
    w_spec = pl.BlockSpec(w_bf.shape, lambda b, s: (0, 0, 0))
    ext = pltpu.VMEM((MAX_WINDOW + SEQ_TILE, D), F32)
    if moe is None:
        return pl.pallas_call(
            _pool_kernel, grid=(B, ns),
            in_specs=[h_spec, vec_spec, w_spec, vec_spec],
            out_specs=h_spec, out_shape=jax.ShapeDtypeStruct(h.shape, F32),
            scratch_shapes=[ext], compiler_params=_cparams(2), name="pool_mixer",
        )(h, norm_g, w_bf, scale)
    gates, pos, y = moe
    any_spec = pl.BlockSpec(memory_space=pl.ANY)
    return pl.pallas_call(
        _pool_combine_kernel, grid=(B, ns),
        in_specs=[h_spec, pl.BlockSpec((SEQ_TILE, LANES), lambda b, s: (b * ns + s, 0)),
                  any_spec, any_spec, vec_spec, w_spec, vec_spec],
        out_specs=h_spec, out_shape=jax.ShapeDtypeStruct(h.shape, F32),
        scratch_shapes=[ext, pltpu.SMEM((2 * SEQ_TILE,), I32),
                        pltpu.VMEM((2, SEQ_TILE, D), F32), pltpu.SemaphoreType.DMA],
        compiler_params=_cparams(2), name="combine_pool_mixer",
    )(h, gates, pos, y, norm_g, w_bf, scale)


def _swiglu_chunk(xn, wg_ref, wu_ref, wd_ref, acc_ref):
    for o in range(0, FF_CHUNK, FF_SUB):
        a = jnp.dot(xn, wg_ref[:, o:o + FF_SUB], preferred_element_type=F32)
        u = jnp.dot(xn, wu_ref[:, o:o + FF_SUB], preferred_element_type=F32)
        hid = (a * jax.nn.sigmoid(a) * u).astype(BF16)
        acc_ref[...] += jnp.dot(hid, wd_ref[o:o + FF_SUB, :], preferred_element_type=F32)


def _ffn_kernel(x_ref, g_ref, wg_ref, wu_ref, wd_ref, o_ref, xn_ref, acc_ref):
    c = pl.program_id(1)

    @pl.when(c == 0)
    def _():
        xn_ref[...] = _rms(x_ref[...], g_ref[...]).astype(BF16)
        acc_ref[...] = jnp.zeros_like(acc_ref)

    _swiglu_chunk(xn_ref[...], wg_ref, wu_ref, wd_ref, acc_ref)

    @pl.when(c == pl.num_programs(1) - 1)
    def _():
        o_ref[...] = x_ref[...] + acc_ref[...]


def _ffn_layer(h2, norm_g, wg, wu, wd):
    N, D = h2.shape
    nc = D_FF // FF_CHUNK
    x_spec = pl.BlockSpec((SEQ_TILE, D), lambda i, c: (i, 0))
    return pl.pallas_call(
        _ffn_kernel, grid=(N // SEQ_TILE, nc),
        in_specs=[x_spec, pl.BlockSpec((1, D), lambda i, c: (0, 0)),
                  pl.BlockSpec((D, FF_CHUNK), lambda i, c: (0, c)),
                  pl.BlockSpec((D, FF_CHUNK), lambda i, c: (0, c)),
                  pl.BlockSpec((FF_CHUNK, D), lambda i, c: (c, 0))],
        out_specs=x_spec, out_shape=jax.ShapeDtypeStruct(h2.shape, F32),
        scratch_shapes=[pltpu.VMEM((SEQ_TILE, D), BF16), pltpu.VMEM((SEQ_TILE, D), F32)],
        compiler_params=_cparams(2), name="dense_swiglu",
    )(h2, norm_g, wg, wu, wd)


def _conv_kernel(h_ref, g_ref, win_ref, ck_ref, wout_ref, gm_ref, r_ref, tri_ref,
                 o_ref, route_ref, cnt_ref, ext_ref, run_ref):
    first = (pl.program_id(0) == 0) & (pl.program_id(1) == 0)

    @pl.when(first)
    def _():
        run_ref[...] = jnp.zeros_like(run_ref)

    @pl.when(pl.program_id(1) == 0)
    def _():
        ext_ref[0:CONV_HALO, :] = jnp.zeros((CONV_HALO, D_MODEL), F32)

    h = h_ref[0]
    xn = _rms(h, g_ref[...]).astype(BF16)
    b = jnp.dot(xn, win_ref[:, 0:D_MODEL], preferred_element_type=F32)
    c = jnp.dot(xn, win_ref[:, D_MODEL:2 * D_MODEL], preferred_element_type=F32)
    v = jnp.dot(xn, win_ref[:, 2 * D_MODEL:], preferred_element_type=F32)
    u = c * v
    ext_ref[CONV_HALO:, :] = u
    e = ext_ref[...]
    ck = ck_ref[...]
    y = (pltpu.roll(e, 2, 0) * ck[0:1] + pltpu.roll(e, 1, 0) * ck[1:2] + e * ck[2:3])[CONV_HALO:, :]
    z = (b * y).astype(BF16)
    hn = h + jnp.dot(z, wout_ref[...], preferred_element_type=F32)
    o_ref[0] = hn
    ext_ref[0:CONV_HALO, :] = u[SEQ_TILE - CONV_HALO:, :]

    xm = _rms(hn, gm_ref[...]).astype(BF16)
    logits = jnp.dot(xm, r_ref[...], preferred_element_type=F32)
    lane = lax.broadcasted_iota(I32, (SEQ_TILE, LANES), 1)
    neg = jnp.float32(-jnp.inf)
    l1 = jnp.where(lane < N_EXPERTS, logits, neg)
    m1 = jnp.max(l1, axis=1, keepdims=True)
    i1 = jnp.min(jnp.where(l1 == m1, lane, LANES), axis=1, keepdims=True)
    l2 = jnp.where(lane == i1, neg, l1)
    m2 = jnp.max(l2, axis=1, keepdims=True)
    i2 = jnp.min(jnp.where(l2 == m2, lane, LANES), axis=1, keepdims=True)
    ex = jnp.exp(m2 - m1)
    w0 = 1.0 / (1.0 + ex)
    w1 = ex / (1.0 + ex)
    sel1 = lane == i1
    sel2 = lane == i2
    onehot = jnp.where(sel1 | sel2, 1.0, 0.0)
    before = jnp.dot(tri_ref[...], onehot.astype(BF16), preferred_element_type=F32) + run_ref[...]
    r0 = jnp.sum(jnp.where(sel1, before, 0.0), axis=1, keepdims=True)
    r1 = jnp.sum(jnp.where(sel2, before, 0.0), axis=1, keepdims=True)
    run_ref[...] += jnp.sum(onehot, axis=0, keepdims=True)
    route = jnp.where(lane == 0, i1.astype(F32), 0.0)
    route = jnp.where(lane == 1, i2.astype(F32), route)
    route = jnp.where(lane == 2, r0, route)
    route = jnp.where(lane == 3, r1, route)
    route = jnp.where(lane == 4, w0, route)
    route = jnp.where(lane == 5, w1, route)
    route_ref[...] = route
    cnt_ref[...] = jnp.broadcast_to(run_ref[...], cnt_ref.shape)


def _conv_layer(h, norm_g, w_in, conv_k, w_out, moe_g, router_pad, tri):
    B, T, D = h.shape
    ns = T // SEQ_TILE
    h_spec = pl.BlockSpec((1, SEQ_TILE, D), lambda b, s: (b, s, 0))
    vec_spec = pl.BlockSpec((1, D), lambda b, s: (0, 0))

    def full(a):
        return pl.BlockSpec(a.shape, lambda b, s: (0,) * a.ndim)

    return pl.pallas_call(
        _conv_kernel, grid=(B, ns),
        in_specs=[h_spec, vec_spec, full(w_in), full(conv_k), full(w_out), vec_spec,
                  full(router_pad), full(tri)],
        out_specs=[h_spec, pl.BlockSpec((SEQ_TILE, LANES), lambda b, s: (b * ns + s, 0)),
                   pl.BlockSpec((8, LANES), lambda b, s: (0, 0))],
        out_shape=[jax.ShapeDtypeStruct(h.shape, F32),
                   jax.ShapeDtypeStruct((B * T, LANES), F32),
                   jax.ShapeDtypeStruct((8, LANES), F32)],
        scratch_shapes=[pltpu.VMEM((CONV_HALO + SEQ_TILE, D), F32), pltpu.VMEM((1, LANES), F32)],
        compiler_params=_cparams(2), name="conv_mixer_router",
    )(h, norm_g, w_in, conv_k, w_out, moe_g, router_pad, tri)


def _dispatch_kernel(h_ref, pos_hbm, xs_init, xs_hbm, pos_smem, sem):
    del xs_init
    pltpu.sync_copy(pos_hbm.at[pl.program_id(0)], pos_smem)

    def row_copy(t, p):
        return pltpu.make_async_copy(h_ref.at[pl.ds(t, 1)], xs_hbm.at[pl.ds(p, 1)], sem)

    def issue(t, c):
        for k in range(2):
            row_copy(t, pos_smem[k * SEQ_TILE + t]).start()
        return c

    lax.fori_loop(0, SEQ_TILE, issue, 0)

    def drain(t, c):
        for k in range(2):
            row_copy(t, 0).wait()
        return c

    lax.fori_loop(0, SEQ_TILE, drain, 0)


def _dispatch(h2, pos, n_rows):
    N, D = h2.shape
    return pl.pallas_call(
        _dispatch_kernel, grid=(N // SEQ_TILE,),
        in_specs=[pl.BlockSpec((SEQ_TILE, D), lambda i: (i, 0)), pl.BlockSpec(memory_space=pl.ANY),
                  pl.BlockSpec(memory_space=pl.ANY)],
        out_specs=pl.BlockSpec(memory_space=pl.ANY),
        out_shape=jax.ShapeDtypeStruct((n_rows, D), F32),
        input_output_aliases={2: 0},
        scratch_shapes=[pltpu.SMEM((2 * SEQ_TILE,), I32), pltpu.SemaphoreType.DMA],
        compiler_params=_cparams(1), name="moe_dispatch",
    )(h2, pos, jnp.zeros((n_rows, D), F32))


def _moe_kernel(te_ref, nv_ref, x_ref, g_ref, wg_ref, wu_ref, wd_ref, o_ref, xn_ref, acc_ref):
    j = pl.program_id(0)
    c = pl.program_id(1)
    nvalid = nv_ref[j]

    @pl.when(c == 0)
    def _():
        row = lax.broadcasted_iota(I32, (MOE_TILE, 1), 0)
        x = jnp.where(row < nvalid, x_ref[...], 0.0)
        xn_ref[...] = _rms(x, g_ref[...]).astype(BF16)
        acc_ref[...] = jnp.zeros_like(acc_ref)

    @pl.when(nvalid > 0)
    def _():
        _swiglu_chunk(xn_ref[...], wg_ref.at[0], wu_ref.at[0], wd_ref.at[0], acc_ref)

    @pl.when(c == pl.num_programs(1) - 1)
    def _():
        o_ref[...] = acc_ref[...]


def _moe_experts(xs, tile_e, tile_nvalid, norm_g, wg, wu, wd):
    R, D = xs.shape
    nc = D_FF // FF_CHUNK
    x_spec = pl.BlockSpec((MOE_TILE, D), lambda j, c, te, nv: (j, 0))
    grid_spec = pltpu.PrefetchScalarGridSpec(
        num_scalar_prefetch=2, grid=(R // MOE_TILE, nc),
        in_specs=[x_spec, pl.BlockSpec((1, D), lambda j, c, te, nv: (0, 0)),
                  pl.BlockSpec((1, D, FF_CHUNK), lambda j, c, te, nv: (te[j], 0, c)),
                  pl.BlockSpec((1, D, FF_CHUNK), lambda j, c, te, nv: (te[j], 0, c)),
                  pl.BlockSpec((1, FF_CHUNK, D), lambda j, c, te, nv: (te[j], c, 0))],
        out_specs=x_spec,
        scratch_shapes=[pltpu.VMEM((MOE_TILE, D), BF16), pltpu.VMEM((MOE_TILE, D), F32)])
    return pl.pallas_call(
        _moe_kernel, grid_spec=grid_spec, out_shape=jax.ShapeDtypeStruct(xs.shape, F32),
        compiler_params=_cparams(2), name="moe_experts",
    )(tile_e, tile_nvalid, xs, norm_g, wg, wu, wd)


def _final_kernel(h_ref, gates_ref, pos_hbm, y_hbm, g_ref, o_ref, pos_smem, ybuf, sem):
    _gather_rows(pl.program_id(0), pos_hbm, y_hbm, pos_smem, ybuf, sem)
    h = _combined(h_ref[...], gates_ref[...], ybuf)
    o_ref[...] = _rms(h, g_ref[...])


def _final_layer(h2, gates, pos, y, norm_g):
    N, D = h2.shape
    x_spec = pl.BlockSpec((SEQ_TILE, D), lambda i: (i, 0))
    any_spec = pl.BlockSpec(memory_space=pl.ANY)
    return pl.pallas_call(
        _final_kernel, grid=(N // SEQ_TILE,),
        in_specs=[x_spec, pl.BlockSpec((SEQ_TILE, LANES), lambda i: (i, 0)), any_spec, any_spec,
                  pl.BlockSpec((1, D), lambda i: (0, 0))],
        out_specs=x_spec, out_shape=jax.ShapeDtypeStruct(h2.shape, F32),
        scratch_shapes=[pltpu.SMEM((2 * SEQ_TILE,), I32), pltpu.VMEM((2, SEQ_TILE, D), F32),
                        pltpu.SemaphoreType.DMA],
        compiler_params=_cparams(1), name="combine_final_norm",
    )(h2, gates, pos, y, norm_g)


def _routing_tables(route, counts, n_tiles):
    N = route.shape[0]
    cnt = counts[0, :N_EXPERTS].astype(I32)
    tiles = (cnt + MOE_TILE - 1) // MOE_TILE
    tile_end = jnp.cumsum(tiles)
    tile_start = tile_end - tiles
    row_start = tile_start * MOE_TILE
    e = route[:, 0:2].astype(I32)
    rank = route[:, 2:4].astype(I32)
    base = jnp.sum(jnp.where(e[:, :, None] == jnp.arange(N_EXPERTS)[None, None, :],
                             row_start[None, None, :], 0), axis=-1)
    pos = (base + rank).T
    pos = pos.reshape(2, N // SEQ_TILE, SEQ_TILE).transpose(1, 0, 2).reshape(N // SEQ_TILE, 2 * SEQ_TILE)
    j = jnp.arange(n_tiles, dtype=I32)
    tile_e = jnp.minimum(jnp.sum(j[:, None] >= tile_end[None, :], axis=1), N_EXPERTS - 1).astype(I32)
    nvalid = jnp.clip(cnt[tile_e] - (j - tile_start[tile_e]) * MOE_TILE, 0, MOE_TILE)
    nvalid = jnp.where(j < tile_end[-1], nvalid, 0).astype(I32)
    return pos.astype(I32), tile_e, nvalid


def kernel(x, meta_tokens, pool_norm, pool_w, pool_scale, dense_norm, dense_w_gate, dense_w_up, dense_w_down, conv_norm, conv_w_in, conv_kernel, conv_w_out, moe_norm, moe_router, moe_w_gate, moe_w_up, moe_w_down, final_norm):
    B, S, D = x.shape
    T = N_META + S
    N = B * T
    assert D == D_MODEL and T % SEQ_TILE == 0 and dense_w_gate.shape[-1] == D_FF
    assert pool_norm.shape[0] == conv_norm.shape[0]
    n_tiles = (2 * N) // MOE_TILE + N_EXPERTS
    n_rows = n_tiles * MOE_TILE

    meta = jnp.broadcast_to(meta_tokens[None].astype(x.dtype), (B, N_META, D))
    h = jnp.concatenate([meta, x], axis=1)

    row = lambda v: v.reshape(1, D)
    tri = (jnp.arange(SEQ_TILE)[:, None] > jnp.arange(SEQ_TILE)[None, :]).astype(BF16)
    moe = None
    depth = pool_norm.shape[0] + conv_norm.shape[0]
    for i in range(depth):
        j = i // 2
        if i % 2 == 0:
            h = _pool_layer(h, row(pool_norm[j]), pool_w[j].astype(BF16), row(pool_scale[j]), moe)
            moe = None
            h = _ffn_layer(h.reshape(N, D), row(dense_norm[j]), dense_w_gate[j].astype(BF16),
                           dense_w_up[j].astype(BF16), dense_w_down[j].astype(BF16)).reshape(B, T, D)
        else:
            router_pad = jnp.zeros((D, LANES), BF16).at[:, :N_EXPERTS].set(moe_router[j].astype(BF16))
            h, route, counts = _conv_layer(h, row(conv_norm[j]), conv_w_in[j].astype(BF16),
                                           conv_kernel[j], conv_w_out[j].astype(BF16),
                                           row(moe_norm[j]), router_pad, tri)
            pos, tile_e, nvalid = _routing_tables(route, counts, n_tiles)
            xs = _dispatch(h.reshape(N, D), pos, n_rows)
            ys = _moe_experts(xs, tile_e, nvalid, row(moe_norm[j]), moe_w_gate[j].astype(BF16),
                              moe_w_up[j].astype(BF16), moe_w_down[j].astype(BF16))
            moe = (route, pos, ys)
    out = _final_layer(h.reshape(N, D), moe[0], moe[1], moe[2], row(final_norm))
    return out.reshape(B, T, D)[:, N_META:]
```
